```python
import math
import jax, jax.numpy as jnp
from jax import lax
import numpy as np

D_MODEL = 2048
BATCH = 4
SEQ = 4096
DEPTH = 1

D_MIX = D_MODEL
D_ATTN = D_MIX // 2
D_SSM = D_MIX - D_ATTN
HEAD_DIM = 64
N_Q_HEADS = D_ATTN // HEAD_DIM
N_KV_HEADS = 2
Q_PER_KV = N_Q_HEADS // N_KV_HEADS
D_KV = N_KV_HEADS * HEAD_DIM
WINDOW = 128
BLOCK = 128
NUM_BUCKETS = 32
MAX_DISTANCE = 128
SSM_GROUP = 16
N_SSM_GROUPS = D_SSM // SSM_GROUP
SSM_STATE = 64
DT_MIN = 0.001
DT_MAX = 0.1
N_EXPERT_GROUPS = 4
EXPERTS_PER_GROUP = 8
TOP_K = 2
D_FF_EXPERT = 512

D_IN_PROJ = D_ATTN + 2 * D_KV + D_SSM
EPS = 1e-6
NEG_INF = -1e30

kernel_name = "hymba_swa_s5_hiermoe_block"


def rmsnorm(x, g):
    x32 = x.astype(jnp.float32)
    y = x32 * lax.rsqrt(jnp.mean(x32 * x32, axis=-1, keepdims=True) + EPS)
    return (y * g.astype(jnp.float32)).astype(x.dtype)


def t5_bucket(n):
    max_exact = NUM_BUCKETS // 2
    nf = jnp.maximum(n, 1).astype(jnp.float32)
    large = max_exact + (jnp.log(nf / max_exact) / math.log(MAX_DISTANCE / max_exact)
                         * (NUM_BUCKETS - max_exact)).astype(jnp.int32)
    large = jnp.minimum(large, NUM_BUCKETS - 1)
    return jnp.where(n < max_exact, n, large)


def band_offsets():
    i = jnp.arange(BLOCK)[:, None]
    j = jnp.arange(2 * BLOCK)[None, :]
    return i + BLOCK - j, j


def t5_block_bias(table):
    dist, _ = band_offsets()
    bias = table[t5_bucket(jnp.maximum(dist, 0))]
    return jnp.transpose(bias, (2, 0, 1)).astype(jnp.float32)


def sliding_window_attention(q, k, v, sinks, bias_table):
    bsz, seq = q.shape[0], q.shape[1]
    nb = seq // BLOCK
    qb = q.reshape(bsz, nb, BLOCK, N_KV_HEADS, Q_PER_KV, HEAD_DIM)

    def band(t):
        prev = jnp.pad(t, ((0, 0), (BLOCK, 0), (0, 0), (0, 0)))[:, :seq]
        prev = prev.reshape(bsz, nb, BLOCK, N_KV_HEADS, HEAD_DIM)
        cur = t.reshape(bsz, nb, BLOCK, N_KV_HEADS, HEAD_DIM)
        return jnp.concatenate([prev, cur], axis=2)

    kb, vb = band(k), band(v)
    scores = jnp.einsum('bnqkgd,bnskd->bnkgqs', qb, kb,
                        preferred_element_type=jnp.float32) * (HEAD_DIM ** -0.5)
    bias = t5_block_bias(bias_table).reshape(N_KV_HEADS, Q_PER_KV, BLOCK, 2 * BLOCK)
    scores = scores + bias
    dist, j = band_offsets()
    kpos = jnp.arange(nb)[:, None, None] * BLOCK - BLOCK + j[None]
    valid = (dist >= 0)[None] & (dist < WINDOW)[None] & (kpos >= 0)
    scores = jnp.where(valid[None, :, None, None], scores, NEG_INF)
    sink = jnp.broadcast_to(sinks.astype(jnp.float32).reshape(N_KV_HEADS, Q_PER_KV, 1, 1),
                            scores.shape[:-1] + (1,))
    probs = jax.nn.softmax(jnp.concatenate([scores, sink], axis=-1), axis=-1)[..., :-1]
    out = jnp.einsum('bnkgqs,bnskd->bnqkgd', probs.astype(vb.dtype), vb)
    return out.reshape(bsz, seq, N_Q_HEADS * HEAD_DIM)


def complex_affine_combine(c1, c2):
    a1r, a1i, b1r, b1i = c1
    a2r, a2i, b2r, b2i = c2
    return (a2r * a1r - a2i * a1i,
            a2r * a1i + a2i * a1r,
            a2r * b1r - a2i * b1i + b2r,
            a2r * b1i + a2i * b1r + b2i)


def s5_mixer(u, lam_re, lam_im, log_dt, b_re, b_im, c_re, c_im, d_skip, w_glu, b_glu):
    f32 = jnp.float32
    bsz, seq = u.shape[0], u.shape[1]
    ug = u.astype(f32).reshape(bsz, seq, N_SSM_GROUPS, SSM_GROUP)
    dt = jnp.exp(log_dt.astype(f32))[:, None]
    lr, li = lam_re.astype(f32), lam_im.astype(f32)
    mag = jnp.exp(lr * dt)
    ar, ai = mag * jnp.cos(li * dt), mag * jnp.sin(li * dt)
    den = lr * lr + li * li
    cr = ((ar - 1.0) * lr + ai * li) / den
    ci = (ai * lr - (ar - 1.0) * li) / den
    bu_r = jnp.einsum('blgh,gph->blgp', ug, b_re.astype(f32))
    bu_i = jnp.einsum('blgh,gph->blgp', ug, b_im.astype(f32))
    br = cr * bu_r - ci * bu_i
    bi = cr * bu_i + ci * bu_r
    a_r = jnp.broadcast_to(ar, br.shape)
    a_i = jnp.broadcast_to(ai, br.shape)
    _, _, xr, xi = lax.associative_scan(complex_affine_combine, (a_r, a_i, br, bi), axis=1)
    y = (jnp.einsum('ghp,blgp->blgh', c_re.astype(f32), xr)
         - jnp.einsum('ghp,blgp->blgh', c_im.astype(f32), xi)
         + d_skip.astype(f32) * ug)
    y = jax.nn.gelu(y.reshape(bsz, seq, D_SSM))
    y = y * jax.nn.sigmoid(y @ w_glu.astype(f32) + b_glu.astype(f32))
    return y.astype(u.dtype)


def hierarchical_moe(h, w_router_group, b_router_group, w_router_expert, b_router_expert,
                     w_gate, w_up, w_down):
    f32 = jnp.float32
    bsz, seq, d = h.shape
    t = h.reshape(bsz * seq, d)
    g_probs = jax.nn.softmax((t @ w_router_group).astype(f32) + b_router_group.astype(f32), axis=-1)
    g_idx = jnp.argmax(g_probs, axis=-1)
    g_gate = jnp.take_along_axis(g_probs, g_idx[:, None], axis=-1)
    e_logits = (jnp.einsum('td,gde->tge', t, w_router_expert).astype(f32)
                + b_router_expert.astype(f32))
    e_sel = jnp.take_along_axis(e_logits, g_idx[:, None, None], axis=1)[:, 0]
    top_vals, top_idx = lax.top_k(e_sel, TOP_K)
    top_w = jax.nn.softmax(top_vals, axis=-1) * g_gate
    expert_w = jnp.sum(jax.nn.one_hot(top_idx, EXPERTS_PER_GROUP, dtype=f32)
                       * top_w[..., None], axis=1)
    group_mask = jax.nn.one_hot(g_idx, N_EXPERT_GROUPS, dtype=f32)
    out = jnp.zeros((bsz * seq, d), f32)
    for g in range(N_EXPERT_GROUPS):
        gate = jnp.einsum('td,edf->tef', t, w_gate[g])
        up = jnp.einsum('td,edf->tef', t, w_up[g])
        wgt = (expert_w * group_mask[:, g:g + 1])[..., None]
        hid = (jax.nn.silu(gate) * up).astype(f32) * wgt
        out = out + jnp.einsum('tef,efd->td', hid.astype(t.dtype), w_down[g],
                               preferred_element_type=f32)
    return out.reshape(bsz, seq, d).astype(h.dtype)


def hybrid_layer(x, rel_bias_table, g_norm_mix, w_in, attn_sinks, lam_re, lam_im, log_dt,
                 ssm_b_re, ssm_b_im, ssm_c_re, ssm_c_im, ssm_d, w_glu, b_glu,
                 g_attn_out, g_ssm_out, w_out, g_norm_ffn, w_router_group, b_router_group,
                 w_router_expert, b_router_expert, w_gate, w_up, w_down):
    bsz, seq, _ = x.shape
    h = rmsnorm(x, g_norm_mix)
    proj = h @ w_in
    q, k, v, u = jnp.split(proj, [D_ATTN, D_ATTN + D_KV, D_ATTN + 2 * D_KV], axis=-1)
    q = q.reshape(bsz, seq, N_Q_HEADS, HEAD_DIM)
    k = k.reshape(bsz, seq, N_KV_HEADS, HEAD_DIM)
    v = v.reshape(bsz, seq, N_KV_HEADS, HEAD_DIM)
    attn = sliding_window_attention(q, k, v, attn_sinks, rel_bias_table)
    ssm = s5_mixer(u, lam_re, lam_im, log_dt, ssm_b_re, ssm_b_im, ssm_c_re, ssm_c_im,
                   ssm_d, w_glu, b_glu)
    mixed = jnp.concatenate([rmsnorm(attn, g_attn_out), rmsnorm(ssm, g_ssm_out)], axis=-1)
    x = x + mixed @ w_out
    x = x + hierarchical_moe(rmsnorm(x, g_norm_ffn), w_router_group, b_router_group,
                             w_router_expert, b_router_expert, w_gate, w_up, w_down)
    return x


def setup_inputs(seed: int = 0) -> dict:
    key = jax.random.key(seed)
    ks = jax.random.split(key, 32)
    f32 = jnp.float32
    nrm = lambda k, shape, scale: jax.random.normal(k, shape, f32) * scale
    L = DEPTH
    G, P, HG = N_SSM_GROUPS, SSM_STATE, SSM_GROUP
    NG, E, F = N_EXPERT_GROUPS, EXPERTS_PER_GROUP, D_FF_EXPERT
    lam_im = (jnp.pi * jnp.arange(P, dtype=f32))[None, None, :] + nrm(ks[6], (L, G, P), 0.01)
    log_dt = jax.random.uniform(ks[7], (L, G), f32, math.log(DT_MIN), math.log(DT_MAX))
    return {
        "x": nrm(ks[0], (BATCH, SEQ, D_MODEL), 1.0),
        "rel_bias_table": nrm(ks[1], (NUM_BUCKETS, N_Q_HEADS), 0.5),
        "g_norm_mix": 1.0 + nrm(ks[2], (L, D_MODEL), 0.02),
        "w_in": nrm(ks[3], (L, D_MODEL, D_IN_PROJ), D_MODEL ** -0.5),
        "attn_sinks": nrm(ks[4], (L, N_Q_HEADS), 0.5),
        "lam_re": -0.5 + nrm(ks[5], (L, G, P), 0.01),
        "lam_im": lam_im,
        "log_dt": log_dt,
        "ssm_b_re": nrm(ks[8], (L, G, P, HG), (2.0 * HG) ** -0.5),
        "ssm_b_im": nrm(ks[9], (L, G, P, HG), (2.0 * HG) ** -0.5),
        "ssm_c_re": nrm(ks[10], (L, G, HG, P), (2.0 * P) ** -0.5),
        "ssm_c_im": nrm(ks[11], (L, G, HG, P), (2.0 * P) ** -0.5),
        "ssm_d": nrm(ks[12], (L, G, HG), 1.0),
        "w_glu": nrm(ks[13], (L, D_SSM, D_SSM), D_SSM ** -0.5),
        "b_glu": nrm(ks[14], (L, D_SSM), 0.02),
        "g_attn_out": 1.0 + nrm(ks[15], (L, D_ATTN), 0.02),
        "g_ssm_out": 1.0 + nrm(ks[16], (L, D_SSM), 0.02),
        "w_out": nrm(ks[17], (L, D_MIX, D_MODEL), D_MIX ** -0.5),
        "g_norm_ffn": 1.0 + nrm(ks[18], (L, D_MODEL), 0.02),
        "w_router_group": nrm(ks[19], (L, D_MODEL, NG), D_MODEL ** -0.5),
        "b_router_group": nrm(ks[20], (L, NG), 0.01),
        "w_router_expert": nrm(ks[21], (L, NG, D_MODEL, E), D_MODEL ** -0.5),
        "b_router_expert": nrm(ks[22], (L, NG, E), 0.01),
        "w_gate": nrm(ks[23], (L, NG, E, D_MODEL, F), D_MODEL ** -0.5),
        "w_up": nrm(ks[24], (L, NG, E, D_MODEL, F), D_MODEL ** -0.5),
        "w_down": nrm(ks[25], (L, NG, E, F, D_MODEL), F ** -0.5),
        "g_final": 1.0 + nrm(ks[26], (D_MODEL,), 0.02),
    }


def reference(x, rel_bias_table, g_norm_mix, w_in, attn_sinks, lam_re, lam_im, log_dt,
              ssm_b_re, ssm_b_im, ssm_c_re, ssm_c_im, ssm_d, w_glu, b_glu,
              g_attn_out, g_ssm_out, w_out, g_norm_ffn, w_router_group, b_router_group,
              w_router_expert, b_router_expert, w_gate, w_up, w_down, g_final):
    for l in range(DEPTH):
        x = hybrid_layer(x, rel_bias_table, g_norm_mix[l], w_in[l], attn_sinks[l],
                         lam_re[l], lam_im[l], log_dt[l], ssm_b_re[l], ssm_b_im[l],
                         ssm_c_re[l], ssm_c_im[l], ssm_d[l], w_glu[l], b_glu[l],
                         g_attn_out[l], g_ssm_out[l], w_out[l], g_norm_ffn[l],
                         w_router_group[l], b_router_group[l], w_router_expert[l],
                         b_router_expert[l], w_gate[l], w_up[l], w_down[l])
    return rmsnorm(x, g_final)
```

```python
import math

import jax
import jax.numpy as jnp
from jax import lax
from jax.experimental import pallas as pl
from jax.experimental.pallas import tpu as pltpu

F32 = jnp.float32
BF16 = jnp.bfloat16
I32 = jnp.int32
U32 = jnp.uint32

EPS = 1e-6
NEG_INF = -1e30

HEAD_DIM = 64
N_Q_HEADS = 16
N_KV_HEADS = 2
Q_PER_KV = N_Q_HEADS // N_KV_HEADS
D_ATTN = N_Q_HEADS * HEAD_DIM
D_KV = N_KV_HEADS * HEAD_DIM
BLOCK = 128
WINDOW = 128
NUM_BUCKETS = 32
MAX_DISTANCE = 128

SSM_GROUP = 16
SSM_STATE = 64
CHUNK = 16
SEGMENTS = 8

N_EXPERT_GROUPS = 4
EXPERTS_PER_GROUP = 8
N_EXPERTS = N_EXPERT_GROUPS * EXPERTS_PER_GROUP
ROUTE_LANES = 128
EXPERT_COL0 = N_EXPERT_GROUPS

V7X_VMEM_LIMIT = 56 * 1024 * 1024

HIGHEST = lax.Precision.HIGHEST


def _rms(x, g):
    ms = jnp.mean(x * x, axis=-1, keepdims=True)
    return x * lax.rsqrt(ms + EPS) * g


def _pack_bf16_pair(lo, hi):
    lo_bits = pltpu.bitcast(lo.astype(BF16).astype(F32), U32) >> 16
    hi_bits = pltpu.bitcast(hi.astype(BF16).astype(F32), U32) & jnp.uint32(0xFFFF0000)
    return lo_bits | hi_bits


def _unpack_bf16_pair(p):
    lo = pltpu.bitcast(p << 16, F32)
    hi = pltpu.bitcast(p & jnp.uint32(0xFFFF0000), F32)
    return lo, hi


def _inproj_kernel(x_ref, g_ref, w_ref, q_ref, kv_ref, u_ref):
    h = _rms(x_ref[...], g_ref[...]).astype(BF16)
    p = jnp.dot(h, w_ref[...], preferred_element_type=F32)
    q_ref[...] = (p[:, :D_ATTN] * (HEAD_DIM ** -0.5)).astype(BF16)
    kv_ref[...] = p[:, D_ATTN:D_ATTN + 2 * D_KV].astype(BF16)
    u_ref[...] = p[:, D_ATTN + 2 * D_KV:].astype(BF16)


def _in_proj(x2, g, w_bf, tm):
    t, d = x2.shape
    n = w_bf.shape[1]
    d_ssm = n - D_ATTN - 2 * D_KV
    return pl.pallas_call(
        _inproj_kernel,
        grid=(t // tm,),
        in_specs=[
            pl.BlockSpec((tm, d), lambda i: (i, 0)),
            pl.BlockSpec((1, d), lambda i: (0, 0)),
            pl.BlockSpec((d, n), lambda i: (0, 0)),
        ],
        out_specs=[
            pl.BlockSpec((tm, D_ATTN), lambda i: (i, 0)),
            pl.BlockSpec((tm, 2 * D_KV), lambda i: (i, 0)),
            pl.BlockSpec((tm, d_ssm), lambda i: (i, 0)),
        ],
        out_shape=[
            jax.ShapeDtypeStruct((t, D_ATTN), BF16),
            jax.ShapeDtypeStruct((t, 2 * D_KV), BF16),
            jax.ShapeDtypeStruct((t, d_ssm), BF16),
        ],
        compiler_params=pltpu.CompilerParams(
            dimension_semantics=("parallel",), vmem_limit_bytes=V7X_VMEM_LIMIT),
        name="in_proj",
    )(x2, g, w_bf)


def _attn_kernel(tbl_ref, sink_ref, bucket_ref, q_ref, kvc_ref, kvp_ref, g_ref, o_ref,
                 bias_scr, acc_scr):
    b = pl.program_id(0)
    i = pl.program_id(1)

    @pl.when((b == 0) & (i == 0))
    def _():
        bucket = bucket_ref[...]
        for h in range(N_Q_HEADS):
            acc = jnp.full((BLOCK, 2 * BLOCK), NEG_INF, F32)
            for bk in range(NUM_BUCKETS):
                acc = jnp.where(bucket == bk, tbl_ref[bk, h], acc)
            bias_scr[h] = acc

    kv = jnp.concatenate([kvp_ref[...], kvc_ref[...]], axis=0)
    col = lax.broadcasted_iota(I32, (BLOCK, 2 * BLOCK), 1)
    no_prev = (col < BLOCK) & (i == 0)
    for h in range(N_Q_HEADS):
        kvh = h // Q_PER_KV
        qh = q_ref[:, h * HEAD_DIM:(h + 1) * HEAD_DIM]
        k = kv[:, kvh * HEAD_DIM:(kvh + 1) * HEAD_DIM]
        v = kv[:, D_KV + kvh * HEAD_DIM:D_KV + (kvh + 1) * HEAD_DIM]
        s = lax.dot_general(qh, k, (((1,), (1,)), ((), ())), preferred_element_type=F32)
        s = jnp.where(no_prev, NEG_INF, s + bias_scr[h])
        sink = sink_ref[h]
        m = jnp.maximum(jnp.max(s, axis=1, keepdims=True), sink)
        p = jnp.exp(s - m)
        l = jnp.sum(p, axis=1, keepdims=True) + jnp.exp(sink - m)
        o = jnp.dot(p.astype(BF16), v, preferred_element_type=F32)
        acc_scr[:, h * HEAD_DIM:(h + 1) * HEAD_DIM] = o * (1.0 / l)
    o_ref[...] = _rms(acc_scr[...], g_ref[...]).astype(BF16)


def _attention(q, kv, tbl, sinks, bucket, g, bsz, seq):
    nb = seq // BLOCK
    q3 = q.reshape(bsz, seq, D_ATTN)
    kv3 = kv.reshape(bsz, seq, 2 * D_KV)
    out = pl.pallas_call(
        _attn_kernel,
        grid=(bsz, nb),
        in_specs=[
            pl.BlockSpec(memory_space=pltpu.SMEM),
            pl.BlockSpec(memory_space=pltpu.SMEM),
            pl.BlockSpec((BLOCK, 2 * BLOCK), lambda b, i: (0, 0)),
            pl.BlockSpec((None, BLOCK, D_ATTN), lambda b, i: (b, i, 0)),
            pl.BlockSpec((None, BLOCK, 2 * D_KV), lambda b, i: (b, i, 0)),
            pl.BlockSpec((None, BLOCK, 2 * D_KV), lambda b, i: (b, jnp.maximum(i - 1, 0), 0)),
            pl.BlockSpec((1, D_ATTN), lambda b, i: (0, 0)),
        ],
        out_specs=pl.BlockSpec((None, BLOCK, D_ATTN), lambda b, i: (b, i, 0)),
        out_shape=jax.ShapeDtypeStruct((bsz, seq, D_ATTN), BF16),
        scratch_shapes=[
            pltpu.VMEM((N_Q_HEADS, BLOCK, 2 * BLOCK), F32),
            pltpu.VMEM((BLOCK, D_ATTN), F32),
        ],
        compiler_params=pltpu.CompilerParams(
            dimension_semantics=("arbitrary", "arbitrary"), vmem_limit_bytes=V7X_VMEM_LIMIT),
        name="attention",
    )(tbl, sinks, bucket, q3, kv3, kv3, g)
    return out.reshape(bsz * seq, D_ATTN)


def _band_buckets():
    i = jnp.arange(BLOCK)[:, None]
    j = jnp.arange(2 * BLOCK)[None, :]
    dist = i + BLOCK - j
    n = jnp.maximum(dist, 0)
    max_exact = NUM_BUCKETS // 2
    nf = jnp.maximum(n, 1).astype(F32)
    large = max_exact + (jnp.log(nf / max_exact) / math.log(MAX_DISTANCE / max_exact)
                         * (NUM_BUCKETS - max_exact)).astype(I32)
    large = jnp.minimum(large, NUM_BUCKETS - 1)
    bucket = jnp.where(n < max_exact, n, large)
    valid = (dist >= 0) & (dist < WINDOW)
    return jnp.where(valid, bucket, -1).astype(I32)


def _cmul_lanes(a1, a2, x):
    return a1 * x + a2 * pltpu.roll(x, SSM_STATE, 1)


def _ssm_kernel(u_ref, mk_ref, mb_ref, mc_ref, c_ref, y_ref, xp_scr):
    rows = u_ref.shape[0]
    per_seg = rows // SEGMENTS
    u = u_ref[...]
    s = jnp.dot(u, mb_ref[...], preferred_element_type=F32)
    a1 = c_ref[0:1, :]
    a2 = c_ref[1:2, :]
    x = s[0:SEGMENTS]
    local = [x]
    for k in range(1, per_seg):
        x = _cmul_lanes(a1, a2, x) + s[k * SEGMENTS:(k + 1) * SEGMENTS]
        local.append(x)
    seg_end = x
    b1 = c_ref[2:3, :]
    b2 = c_ref[3:4, :]
    sub = lax.broadcasted_iota(I32, (SEGMENTS, 2 * SSM_STATE), 0)
    carry = jnp.zeros((SEGMENTS, 2 * SSM_STATE), F32)
    for _ in range(SEGMENTS - 1):
        t = _cmul_lanes(b1, b2, carry) + seg_end
        carry = jnp.where(sub == 0, 0.0, pltpu.roll(t, 1, 0))
    xp_scr[0:SEGMENTS, :] = carry
    for k in range(1, per_seg):
        p1 = c_ref[8 + 2 * (k - 1):9 + 2 * (k - 1), :]
        p2 = c_ref[9 + 2 * (k - 1):10 + 2 * (k - 1), :]
        xp_scr[k * SEGMENTS:(k + 1) * SEGMENTS, :] = local[k - 1] + _cmul_lanes(p1, p2, carry)
    y = jnp.dot(u, mk_ref[...], preferred_element_type=F32)
    y = y + jnp.dot(xp_scr[...].astype(BF16), mc_ref[...], preferred_element_type=F32)
    y_ref[...] = y


def _ssm_core(u_t, mk, mb, mc, consts):
    g, bsz, rows, width = u_t.shape
    return pl.pallas_call(
        _ssm_kernel,
        grid=(g, bsz),
        in_specs=[
            pl.BlockSpec((None, None, rows, width), lambda gi, b: (gi, b, 0, 0)),
            pl.BlockSpec((None, width, width), lambda gi, b: (gi, 0, 0)),
            pl.BlockSpec((None, width, 2 * SSM_STATE), lambda gi, b: (gi, 0, 0)),
            pl.BlockSpec((None, 2 * SSM_STATE, width), lambda gi, b: (gi, 0, 0)),
            pl.BlockSpec((None, consts.shape[1], 2 * SSM_STATE), lambda gi, b: (gi, 0, 0)),
        ],
        out_specs=pl.BlockSpec((None, None, rows, width), lambda gi, b: (gi, b, 0, 0)),
        out_shape=jax.ShapeDtypeStruct((g, bsz, rows, width), F32),
        scratch_shapes=[pltpu.VMEM((rows, 2 * SSM_STATE), F32)],
        compiler_params=pltpu.CompilerParams(
            dimension_semantics=("parallel", "parallel"), vmem_limit_bytes=V7X_VMEM_LIMIT),
        name="ssm_scan",
    )(u_t, mk, mb, mc, consts)


def _ssm_operators(lam_re, lam_im, log_dt, b_re, b_im, c_re, c_im, per_seg):
    g, p = lam_re.shape
    h = b_re.shape[-1]
    dt = jnp.exp(log_dt.astype(F32))[:, None]
    lr, li = lam_re.astype(F32), lam_im.astype(F32)
    mag = jnp.exp(lr * dt)
    ar, ai = mag * jnp.cos(li * dt), mag * jnp.sin(li * dt)
    den = lr * lr + li * li
    cr = ((ar - 1.0) * lr + ai * li) / den
    ci = (ai * lr - (ar - 1.0) * li) / den
    bbr = cr[:, :, None] * b_re - ci[:, :, None] * b_im
    bbi = cr[:, :, None] * b_im + ci[:, :, None] * b_re
    ccr, cci = c_re.astype(F32), c_im.astype(F32)

    def cmul(xr, xi, yr, yi):
        return xr * yr - xi * yi, xr * yi + xi * yr

    pw = [(jnp.ones_like(ar), jnp.zeros_like(ai))]
    for _ in range(CHUNK):
        pw.append(cmul(pw[-1][0], pw[-1][1], ar, ai))
    pwr = jnp.stack([x[0] for x in pw])
    pwi = jnp.stack([x[1] for x in pw])

    wr = ccr[None] * pwr[:CHUNK, :, None, :] - cci[None] * pwi[:CHUNK, :, None, :]
    wi = ccr[None] * pwi[:CHUNK, :, None, :] + cci[None] * pwr[:CHUNK, :, None, :]
    kj = (jnp.einsum("jgip,gph->jgih", wr, bbr, precision=HIGHEST)
          - jnp.einsum("jgip,gph->jgih", wi, bbi, precision=HIGHEST))
    lag = jnp.arange(CHUNK)[None, :] - jnp.arange(CHUNK)[:, None]
    blocks = jnp.where((lag >= 0)[:, :, None, None, None], kj[jnp.clip(lag, 0, CHUNK - 1)], 0.0)
    mk = jnp.transpose(blocks, (2, 0, 4, 1, 3)).reshape(g, CHUNK * h, CHUNK * h)

    rev_r, rev_i = pwr[CHUNK - 1::-1], pwi[CHUNK - 1::-1]
    mbr = rev_r[:, :, :, None] * bbr[None] - rev_i[:, :, :, None] * bbi[None]
    mbi = rev_r[:, :, :, None] * bbi[None] + rev_i[:, :, :, None] * bbr[None]
    mb = jnp.concatenate([jnp.transpose(mbr, (1, 0, 3, 2)).reshape(g, CHUNK * h, p),
                          jnp.transpose(mbi, (1, 0, 3, 2)).reshape(g, CHUNK * h, p)], axis=-1)

    vr = ccr[None] * pwr[1:, :, None, :] - cci[None] * pwi[1:, :, None, :]
    vi = ccr[None] * pwi[1:, :, None, :] + cci[None] * pwr[1:, :, None, :]
    mc = jnp.concatenate([jnp.transpose(vr, (1, 3, 0, 2)).reshape(g, p, CHUNK * h),
                          -jnp.transpose(vi, (1, 3, 0, 2)).reshape(g, p, CHUNK * h)], axis=1)

    def lanes(xr, xi):
        return jnp.stack([jnp.concatenate([xr, xr], -1), jnp.concatenate([-xi, xi], -1)], axis=1)

    a_chunk = (pwr[CHUNK], pwi[CHUNK])
    steps = [a_chunk]
    for _ in range(per_seg - 1):
        steps.append(cmul(steps[-1][0], steps[-1][1], a_chunk[0], a_chunk[1]))
    rows = [lanes(*a_chunk), lanes(*steps[-1]), jnp.zeros((g, 4, 2 * p), F32)]
    rows += [lanes(*steps[k]) for k in range(per_seg - 1)]
    consts = jnp.concatenate(rows, axis=1)
    pad = (-consts.shape[1]) % 8
    consts = jnp.pad(consts, ((0, 0), (0, pad), (0, 0)))
    return mk.astype(BF16), mb.astype(BF16), mc.astype(BF16), consts


def _ssmout_kernel(y_ref, u_ref, d_ref, w_ref, b_ref, g_ref, o_ref):
    y = y_ref[...] + d_ref[...] * u_ref[...].astype(F32)
    z = jax.nn.gelu(y)
    gate = jax.nn.sigmoid(jnp.dot(z.astype(BF16), w_ref[...], preferred_element_type=F32) + b_ref[...])
    o_ref[...] = _rms(z * gate, g_ref[...]).astype(BF16)


def _ssm_out(y, u, d, w_bf, b, g, tm):
    t, n = y.shape
    row = lambda i: (i, 0)
    fixed = lambda i: (0, 0)
    return pl.pallas_call(
        _ssmout_kernel,
        grid=(t // tm,),
        in_specs=[
            pl.BlockSpec((tm, n), row), pl.BlockSpec((tm, n), row), pl.BlockSpec((1, n), fixed),
            pl.BlockSpec((n, n), fixed), pl.BlockSpec((1, n), fixed), pl.BlockSpec((1, n), fixed),
        ],
        out_specs=pl.BlockSpec((tm, n), row),
        out_shape=jax.ShapeDtypeStruct((t, n), BF16),
        compiler_params=pltpu.CompilerParams(
            dimension_semantics=("parallel",), vmem_limit_bytes=V7X_VMEM_LIMIT),
        name="ssm_out",
    )(y, u, d, w_bf, b, g)


def _outproj_kernel(x_ref, a_ref, s_ref, wa_ref, ws_ref, g_ref, wr_ref, br_ref,
                    x1_ref, hp_ref, info_ref, wts_ref, cnt_ref, base_scr):
    step = pl.program_id(0)
    tm = x_ref.shape[0]
    d = x_ref.shape[1]

    @pl.when(step == 0)
    def _():
        base_scr[...] = jnp.zeros_like(base_scr)

    x1 = (x_ref[...]
          + jnp.dot(a_ref[...], wa_ref[...], preferred_element_type=F32)
          + jnp.dot(s_ref[...], ws_ref[...], preferred_element_type=F32))
    x1_ref[...] = x1
    h = _rms(x1, g_ref[...])
    hp_ref[...] = _pack_bf16_pair(h[:, :d // 2], h[:, d // 2:])

    lg = jnp.dot(h, wr_ref[...], preferred_element_type=F32, precision=HIGHEST) + br_ref[...]
    col = lax.broadcasted_iota(I32, (tm, ROUTE_LANES), 1)
    colf = col.astype(F32)
    far = float(ROUTE_LANES)

    is_group = col < N_EXPERT_GROUPS
    glog = jnp.where(is_group, lg, -jnp.inf)
    gmax = jnp.max(glog, axis=1, keepdims=True)
    gsum = jnp.sum(jnp.where(is_group, jnp.exp(lg - gmax), 0.0), axis=1, keepdims=True)
    g_gate = 1.0 / gsum
    gidx = jnp.min(jnp.where(glog == gmax, colf, far), axis=1, keepdims=True)
    lo = EXPERT_COL0 + EXPERTS_PER_GROUP * gidx
    in_group = (colf >= lo) & (colf < lo + EXPERTS_PER_GROUP)
    el = jnp.where(in_group, lg, -jnp.inf)
    v1 = jnp.max(el, axis=1, keepdims=True)
    i1 = jnp.min(jnp.where(el == v1, colf, far), axis=1, keepdims=True)
    el2 = jnp.where(colf == i1, -jnp.inf, el)
    v2 = jnp.max(el2, axis=1, keepdims=True)
    i2 = jnp.min(jnp.where(el2 == v2, colf, far), axis=1, keepdims=True)
    ex = jnp.exp(v2 - v1)
    w1 = g_gate / (1.0 + ex)
    w2 = g_gate * ex / (1.0 + ex)

    hit1 = colf == i1
    hit2 = colf == i2
    onehot = jnp.where(hit1 | hit2, 1.0, 0.0)
    r_i = lax.broadcasted_iota(I32, (tm, tm), 0)
    c_i = lax.broadcasted_iota(I32, (tm, tm), 1)
    earlier = jnp.where(c_i < r_i, 1.0, 0.0).astype(BF16)
    before = jnp.dot(earlier, onehot.astype(BF16), preferred_element_type=F32) + base_scr[0:1, :]
    r1 = jnp.sum(jnp.where(hit1, before, 0.0), axis=1, keepdims=True)
    r2 = jnp.sum(jnp.where(hit2, before, 0.0), axis=1, keepdims=True)
    total = base_scr[0:1, :] + jnp.sum(onehot, axis=0, keepdims=True)
    base_scr[...] = jnp.broadcast_to(total, base_scr.shape)
    cnt_ref[...] = jnp.broadcast_to(total, cnt_ref.shape).astype(I32)

    e1 = i1 - EXPERT_COL0
    e2 = i2 - EXPERT_COL0
    info = jnp.where(col == 0, e1, jnp.where(col == 1, e2, jnp.where(col == 2, r1, jnp.where(col == 3, r2, 0.0))))
    info_ref[...] = info.astype(I32)
    wts_ref[...] = jnp.where(col == 0, w1, jnp.where(col == 1, w2, 0.0))


def _out_proj(x2, a, s, wa_bf, ws_bf, g, w_route, b_route, tm):
    t, d = x2.shape
    row = lambda i: (i, 0)
    fixed = lambda i: (0, 0)
    return pl.pallas_call(
        _outproj_kernel,
        grid=(t // tm,),
        in_specs=[
            pl.BlockSpec((tm, d), row),
            pl.BlockSpec((tm, a.shape[1]), row),
            pl.BlockSpec((tm, s.shape[1]), row),
            pl.BlockSpec(wa_bf.shape, fixed),
            pl.BlockSpec(ws_bf.shape, fixed),
            pl.BlockSpec((1, d), fixed),
            pl.BlockSpec((d, ROUTE_LANES), fixed),
            pl.BlockSpec((1, ROUTE_LANES), fixed),
        ],
        out_specs=[
            pl.BlockSpec((tm, d), row),
            pl.BlockSpec((tm, d // 2), row),
            pl.BlockSpec((tm, ROUTE_LANES), row),
            pl.BlockSpec((tm, ROUTE_LANES), row),
            pl.BlockSpec((8, ROUTE_LANES), fixed),
        ],
        out_shape=[
            jax.ShapeDtypeStruct((t, d), F32),
            jax.ShapeDtypeStruct((t, d // 2), U32),
            jax.ShapeDtypeStruct((t, ROUTE_LANES), I32),
            jax.ShapeDtypeStruct((t, ROUTE_LANES), F32),
            jax.ShapeDtypeStruct((8, ROUTE_LANES), I32),
        ],
        scratch_shapes=[pltpu.VMEM((8, ROUTE_LANES), F32)],
        compiler_params=pltpu.CompilerParams(
            dimension_semantics=("arbitrary",), vmem_limit_bytes=V7X_VMEM_LIMIT),
        name="out_proj_route",
    )(x2, a, s, wa_bf, ws_bf, g, w_route, b_route)


def _row_copy(src_ref, src_row, dst_ref, dst_row, sem):
    return pltpu.make_async_copy(src_ref.at[pl.ds(src_row, 1)], dst_ref.at[pl.ds(dst_row, 1)], sem)


def _dispatch_kernel(pos_ref, h_ref, xs_ref, sem):
    tm = h_ref.shape[0]

    def start(r, c):
        _row_copy(h_ref, r, xs_ref, pos_ref[0, 0, r], sem).start()
        _row_copy(h_ref, r, xs_ref, pos_ref[0, 0, tm + r], sem).start()
        return c

    lax.fori_loop(0, tm, start, 0)

    def wait(r, c):
        _row_copy(h_ref, r, xs_ref, pos_ref[0, 0, r], sem).wait()
        _row_copy(h_ref, r, xs_ref, pos_ref[0, 0, tm + r], sem).wait()
        return c

    lax.fori_loop(0, tm, wait, 0)


def _dispatch(pos_blocks, hp, tm):
    t, w = hp.shape
    return pl.pallas_call(
        _dispatch_kernel,
        grid=(t // tm,),
        in_specs=[
            pl.BlockSpec((1, 1, 2 * tm), lambda i: (i, 0, 0), memory_space=pltpu.SMEM),
            pl.BlockSpec((tm, w), lambda i: (i, 0)),
        ],
        out_specs=pl.BlockSpec(memory_space=pl.ANY),
        out_shape=jax.ShapeDtypeStruct((2 * t, w), U32),
        scratch_shapes=[pltpu.SemaphoreType.DMA],
        compiler_params=pltpu.CompilerParams(
            dimension_semantics=("arbitrary",), vmem_limit_bytes=V7X_VMEM_LIMIT),
        name="dispatch",
    )(pos_blocks, hp)


def _experts_kernel(tile_ref, exp_ref, off_ref, n_ref, xs_ref, wg_ref, wu_ref, wd_ref, y_ref,
                    wg_bf, wu_bf, wd_bf):
    w = pl.program_id(0)
    tm = xs_ref.shape[0]
    half = xs_ref.shape[1]
    e = exp_ref[w]
    tile = tile_ref[w]
    prev = jnp.maximum(w - 1, 0)
    new_expert = (w == 0) | (e != exp_ref[prev])
    new_tile = (w == 0) | (tile != tile_ref[prev])
    live = w < n_ref[0]

    @pl.when(new_expert & live)
    def _():
        wg_bf[...] = wg_ref[...].astype(BF16)
        wu_bf[...] = wu_ref[...].astype(BF16)
        wd_bf[...] = wd_ref[...].astype(BF16)

    @pl.when(live)
    def _():
        lo, hi = _unpack_bf16_pair(xs_ref[...])
        lo = lo.astype(BF16)
        hi = hi.astype(BF16)
        gate = (jnp.dot(lo, wg_bf[:half, :], preferred_element_type=F32)
                + jnp.dot(hi, wg_bf[half:, :], preferred_element_type=F32))
        up = (jnp.dot(lo, wu_bf[:half, :], preferred_element_type=F32)
              + jnp.dot(hi, wu_bf[half:, :], preferred_element_type=F32))
        hid = (jax.nn.silu(gate) * up).astype(BF16)
        y = jnp.dot(hid, wd_bf[...], preferred_element_type=F32)
        packed = _pack_bf16_pair(y[:, :half], y[:, half:])

        @pl.when(new_tile)
        def _():
            y_ref[...] = packed

        @pl.when(jnp.logical_not(new_tile))
        def _():
            row = tile * tm + lax.broadcasted_iota(I32, (tm, half), 0)
            mine = (row >= off_ref[e]) & (row < off_ref[e + 1])
            y_ref[...] = jnp.where(mine, packed, y_ref[...])


def _experts(item_tile, item_expert, offsets, n_items, xs, w_gate, w_up, w_down, tm):
    rows, half = xs.shape
    n_e, d, f = w_gate.shape
    grid_spec = pltpu.PrefetchScalarGridSpec(
        num_scalar_prefetch=4,
        grid=(item_tile.shape[0],),
        in_specs=[
            pl.BlockSpec((tm, half), lambda w, tl, ex, of, n: (tl[w], 0)),
            pl.BlockSpec((None, d, f), lambda w, tl, ex, of, n: (ex[w], 0, 0)),
            pl.BlockSpec((None, d, f), lambda w, tl, ex, of, n: (ex[w], 0, 0)),
            pl.BlockSpec((None, f, d), lambda w, tl, ex, of, n: (ex[w], 0, 0)),
        ],
        out_specs=pl.BlockSpec((tm, half), lambda w, tl, ex, of, n: (tl[w], 0)),
        scratch_shapes=[
            pltpu.VMEM((d, f), BF16), pltpu.VMEM((d, f), BF16), pltpu.VMEM((f, d), BF16),
        ],
    )
    return pl.pallas_call(
        _experts_kernel,
        grid_spec=grid_spec,
        out_shape=jax.ShapeDtypeStruct((rows, half), U32),
        compiler_params=pltpu.CompilerParams(
            dimension_semantics=("arbitrary",), vmem_limit_bytes=V7X_VMEM_LIMIT),
        name="experts",
    )(item_tile, item_expert, offsets, n_items, xs, w_gate, w_up, w_down)


def _work_items(counts, tm, n_tiles):
    n_items = n_tiles + N_EXPERTS - 1
    ends = jnp.cumsum(counts)
    offsets = jnp.concatenate([jnp.zeros((1,), I32), ends]).astype(I32)
    starts = offsets[:-1]
    first_tile = starts // tm
    last_tile = jnp.maximum(ends - 1, 0) // tm
    per_expert = jnp.where(counts > 0, last_tile - first_tile + 1, 0)
    item_end = jnp.cumsum(per_expert)
    item_start = item_end - per_expert
    total = item_end[-1]
    w = jnp.arange(n_items, dtype=I32)
    wc = jnp.minimum(w, total - 1)
    expert = jnp.sum((wc[:, None] >= item_end[None, :]).astype(I32), axis=1)
    tile = first_tile[expert] + (wc - item_start[expert])
    return tile.astype(I32), expert.astype(I32), offsets, total.reshape(1).astype(I32)


def _combine_kernel(pos_ref, x1_ref, wts_ref, g_ref, ys_ref, o_ref, buf, sem):
    tm = x1_ref.shape[0]
    half = buf.shape[2]

    def start(r, c):
        _row_copy(ys_ref, pos_ref[0, 0, r], buf.at[0], r, sem).start()
        _row_copy(ys_ref, pos_ref[0, 0, tm + r], buf.at[1], r, sem).start()
        return c

    lax.fori_loop(0, tm, start, 0)

    def wait(r, c):
        _row_copy(ys_ref, pos_ref[0, 0, r], buf.at[0], r, sem).wait()
        _row_copy(ys_ref, pos_ref[0, 0, tm + r], buf.at[1], r, sem).wait()
        return c

    lax.fori_loop(0, tm, wait, 0)

    w1 = wts_ref[:, 0:1]
    w2 = wts_ref[:, 1:2]
    lo1, hi1 = _unpack_bf16_pair(buf[0])
    lo2, hi2 = _unpack_bf16_pair(buf[1])
    x1 = x1_ref[...]
    lo = x1[:, :half] + w1 * lo1 + w2 * lo2
    hi = x1[:, half:] + w1 * hi1 + w2 * hi2
    ms = (jnp.sum(lo * lo, axis=1, keepdims=True) + jnp.sum(hi * hi, axis=1, keepdims=True)) / (2 * half)
    inv = lax.rsqrt(ms + EPS)
    o_ref[:, :half] = lo * inv * g_ref[:, :half]
    o_ref[:, half:] = hi * inv * g_ref[:, half:]


def _combine(pos_blocks, x1, wts, g, ys, tm):
    t, d = x1.shape
    return pl.pallas_call(
        _combine_kernel,
        grid=(t // tm,),
        in_specs=[
            pl.BlockSpec((1, 1, 2 * tm), lambda i: (i, 0, 0), memory_space=pltpu.SMEM),
            pl.BlockSpec((tm, d), lambda i: (i, 0)),
            pl.BlockSpec((tm, ROUTE_LANES), lambda i: (i, 0)),
            pl.BlockSpec((1, d), lambda i: (0, 0)),
            pl.BlockSpec(memory_space=pl.ANY),
        ],
        out_specs=pl.BlockSpec((tm, d), lambda i: (i, 0)),
        out_shape=jax.ShapeDtypeStruct((t, d), F32),
        scratch_shapes=[pltpu.VMEM((2, tm, d // 2), U32), pltpu.SemaphoreType.DMA],
        compiler_params=pltpu.CompilerParams(
            dimension_semantics=("arbitrary",), vmem_limit_bytes=V7X_VMEM_LIMIT),
        name="combine",
    )(pos_blocks, x1, wts, g, ys)


def _layer(x, rel_bias_table, g_norm_mix, w_in, attn_sinks, lam_re, lam_im, log_dt,
           ssm_b_re, ssm_b_im, ssm_c_re, ssm_c_im, ssm_d, w_glu, b_glu, g_attn_out, g_ssm_out,
           w_out, g_norm_ffn, w_router_group, b_router_group, w_router_expert, b_router_expert,
           w_gate, w_up, w_down, g_final):
    bsz, seq, d = x.shape
    t = bsz * seq
    x2 = x.reshape(t, d)
    n_groups = lam_re.shape[0]
    d_ssm = n_groups * SSM_GROUP
    rows = seq // CHUNK
    per_seg = rows // SEGMENTS
    tm = 256

    q, kv, u = _in_proj(x2, g_norm_mix.reshape(1, d), w_in.astype(BF16), tm=512)

    attn = _attention(q, kv, rel_bias_table.astype(F32), attn_sinks.astype(F32), _band_buckets(),
                      g_attn_out.reshape(1, D_ATTN).astype(F32), bsz, seq)

    mk, mb, mc, consts = _ssm_operators(lam_re, lam_im, log_dt, ssm_b_re, ssm_b_im,
                                        ssm_c_re, ssm_c_im, per_seg)
    u_t = u.reshape(bsz, SEGMENTS, per_seg, CHUNK, n_groups, SSM_GROUP)
    u_t = jnp.transpose(u_t, (4, 0, 2, 1, 3, 5)).reshape(n_groups, bsz, rows, CHUNK * SSM_GROUP)
    y_t = _ssm_core(u_t, mk, mb, mc, consts)
    y = y_t.reshape(n_groups, bsz, per_seg, SEGMENTS, CHUNK, SSM_GROUP)
    y = jnp.transpose(y, (1, 3, 2, 4, 0, 5)).reshape(t, d_ssm)
    ssm = _ssm_out(y, u, ssm_d.reshape(1, d_ssm).astype(F32), w_glu.astype(BF16),
                   b_glu.reshape(1, d_ssm).astype(F32), g_ssm_out.reshape(1, d_ssm).astype(F32), tm=512)

    w_out_bf = w_out.astype(BF16)
    n_route = N_EXPERT_GROUPS + N_EXPERTS
    w_route = jnp.concatenate(
        [w_router_group, jnp.transpose(w_router_expert, (1, 0, 2)).reshape(d, N_EXPERTS)], axis=1)
    w_route = jnp.pad(w_route.astype(F32), ((0, 0), (0, ROUTE_LANES - n_route)))
    b_route = jnp.concatenate([b_router_group, b_router_expert.reshape(N_EXPERTS)])
    b_route = jnp.pad(b_route.astype(F32), (0, ROUTE_LANES - n_route)).reshape(1, ROUTE_LANES)
    x1, hp, info, wts, cnt = _out_proj(x2, attn, ssm, w_out_bf[:D_ATTN], w_out_bf[D_ATTN:],
                                       g_norm_ffn.reshape(1, d), w_route, b_route, tm)

    counts = cnt[0, EXPERT_COL0:EXPERT_COL0 + N_EXPERTS]
    n_tiles = (2 * t) // tm
    item_tile, item_expert, offsets, n_items = _work_items(counts, tm, n_tiles)
    pos = offsets[info[:, 0:2]] + info[:, 2:4]
    pos_blocks = jnp.transpose(pos.reshape(t // tm, tm, 2), (0, 2, 1)).reshape(t // tm, 1, 2 * tm)

    xs = _dispatch(pos_blocks, hp, tm)
    ys = _experts(item_tile, item_expert, offsets, n_items, xs,
                  w_gate.reshape(N_EXPERTS, d, -1), w_up.reshape(N_EXPERTS, d, -1),
                  w_down.reshape(N_EXPERTS, -1, d), tm)
    out = _combine(pos_blocks, x1, wts, g_final.reshape(1, d).astype(F32), ys, tm)
    return out.reshape(bsz, seq, d)


def kernel(x, rel_bias_table, g_norm_mix, w_in, attn_sinks, lam_re, lam_im, log_dt, ssm_b_re, ssm_b_im, ssm_c_re, ssm_c_im, ssm_d, w_glu, b_glu, g_attn_out, g_ssm_out, w_out, g_norm_ffn, w_router_group, b_router_group, w_router_expert, b_router_expert, w_gate, w_up, w_down, g_final):
    assert g_norm_mix.shape[0] == 1, "one layer"
    return _layer(x, rel_bias_table, g_norm_mix[0], w_in[0], attn_sinks[0], lam_re[0], lam_im[0],
                  log_dt[0], ssm_b_re[0], ssm_b_im[0], ssm_c_re[0], ssm_c_im[0], ssm_d[0], w_glu[0],
                  b_glu[0], g_attn_out[0], g_ssm_out[0], w_out[0], g_norm_ffn[0], w_router_group[0],
                  b_router_group[0], w_router_expert[0], b_router_expert[0], w_gate[0], w_up[0],
                  w_down[0], g_final)
```
